```python
import jax, jax.numpy as jnp
from jax import lax
import numpy as np

D_MODEL = 1024
BATCH = 8
SEQ = 4096
DEPTH = 1

CHUNK = 64
N_LEFT_CHUNKS = 8
HEAD_DIM = 64
N_HEADS_A = 8
N_HEADS_B = 8
WIDTH_A = N_HEADS_A * HEAD_DIM
WIDTH_B = N_HEADS_B * HEAD_DIM
MIX_WIDTH = WIDTH_A + WIDTH_B
MAX_REL = 128
Q_BLOCK = 128
D_FF = 2816
CONV_WIDTH = 3
EPS = 1e-6
NEG_INF = -1e30

kernel_name = "hybrid_chunked_stickbreaking_convffn"


def rmsnorm(x, g):
    xf = x.astype(jnp.float32)
    y = xf * lax.rsqrt(jnp.mean(xf * xf, axis=-1, keepdims=True) + EPS)
    return (y * g.astype(jnp.float32)).astype(x.dtype)


def chunked_attention(q, k, v, bias_table):
    B, S, H, D = q.shape
    nc = S // CHUNK
    band = (N_LEFT_CHUNKS + 1) * CHUNK
    qc = q.reshape(B, nc, CHUNK, H, D)

    def gather_band(t):
        tc = t.reshape(B, nc, CHUNK, H, D)
        tp = jnp.pad(tc, ((0, 0), (N_LEFT_CHUNKS, 0), (0, 0), (0, 0), (0, 0)))
        return jnp.concatenate([tp[:, j:j + nc] for j in range(N_LEFT_CHUNKS + 1)], axis=2)

    kb = gather_band(k)
    vb = gather_band(v)
    scores = jnp.einsum('bnqhd,bnkhd->bnhqk', qc, kb).astype(jnp.float32) * (D ** -0.5)
    qi = jnp.arange(CHUNK)
    kj = jnp.arange(band)
    dist = N_LEFT_CHUNKS * CHUNK + qi[:, None] - kj[None, :]
    rel_idx = jnp.clip(dist, -MAX_REL, MAX_REL) + MAX_REL
    bias = bias_table[:, rel_idx].astype(jnp.float32)
    key_chunk = jnp.arange(nc)[:, None] - N_LEFT_CHUNKS + (kj // CHUNK)[None, :]
    valid = key_chunk >= 0
    scores = jnp.where(valid[None, :, None, None, :], scores + bias[None, None], NEG_INF)
    p = jax.nn.softmax(scores, axis=-1).astype(v.dtype)
    o = jnp.einsum('bnhqk,bnkhd->bnqhd', p, vb)
    return o.reshape(B, S, H * D)


def stick_breaking_attention(q, k, v):
    B, S, H, D = q.shape
    scale = D ** -0.5
    outs = []
    for blk in range(S // Q_BLOCK):
        q0 = blk * Q_BLOCK
        kv_len = q0 + Q_BLOCK
        qb = q[:, q0:kv_len]
        kb = k[:, :kv_len]
        vb = v[:, :kv_len]
        z = jnp.einsum('bqhd,bkhd->bhqk', qb, kb).astype(jnp.float32) * scale
        t_idx = q0 + jnp.arange(Q_BLOCK)[:, None]
        s_idx = jnp.arange(kv_len)[None, :]
        causal = s_idx < t_idx
        log_beta = jax.nn.log_sigmoid(z)
        log_1m = jnp.where(causal, log_beta - z, 0.0)
        suffix = lax.cumsum(log_1m, axis=log_1m.ndim - 1, reverse=True) - log_1m
        a = jnp.where(causal, jnp.exp(log_beta + suffix), 0.0).astype(v.dtype)
        outs.append(jnp.einsum('bhqk,bkhd->bqhd', a, vb))
    o = jnp.concatenate(outs, axis=1)
    return o.reshape(B, S, H * D)


def conv_gated_mlp(h, w_ffn_in, conv_w, conv_b, w_ffn_out):
    u = h @ w_ffn_in
    c = u.shape[-1]
    u = lax.conv_general_dilated(
        u, conv_w[:, None, :], window_strides=(1,), padding=[(CONV_WIDTH - 1, 0)],
        dimension_numbers=('NWC', 'WIO', 'NWC'), feature_group_count=c) + conv_b
    gate, val = jnp.split(u, 2, axis=-1)
    return (jax.nn.silu(gate) * val) @ w_ffn_out


def setup_inputs(seed: int = 0) -> dict:
    key = jax.random.key(seed)
    ks = jax.random.split(key, 16)
    f32 = jnp.float32
    x = jax.random.normal(ks[0], (BATCH, SEQ, D_MODEL), f32)
    norm1_g = 1.0 + 0.02 * jax.random.normal(ks[1], (DEPTH, D_MODEL), f32)
    w_in = jax.random.normal(ks[2], (DEPTH, D_MODEL, 3 * MIX_WIDTH), f32) * D_MODEL ** -0.5
    rel_bias = 0.1 * jax.random.normal(ks[3], (DEPTH, N_HEADS_A, 2 * MAX_REL + 1), f32)
    norm_a_g = 1.0 + 0.02 * jax.random.normal(ks[4], (DEPTH, WIDTH_A), f32)
    norm_b_g = 1.0 + 0.02 * jax.random.normal(ks[5], (DEPTH, WIDTH_B), f32)
    w_out = jax.random.normal(ks[6], (DEPTH, MIX_WIDTH, D_MODEL), f32) * MIX_WIDTH ** -0.5
    norm2_g = 1.0 + 0.02 * jax.random.normal(ks[7], (DEPTH, D_MODEL), f32)
    w_ffn_in = jax.random.normal(ks[8], (DEPTH, D_MODEL, 2 * D_FF), f32) * D_MODEL ** -0.5
    conv_w = jax.random.normal(ks[9], (DEPTH, CONV_WIDTH, 2 * D_FF), f32) * CONV_WIDTH ** -0.5
    conv_b = 0.01 * jax.random.normal(ks[10], (DEPTH, 2 * D_FF), f32)
    w_ffn_out = jax.random.normal(ks[11], (DEPTH, D_FF, D_MODEL), f32) * D_FF ** -0.5
    final_g = 1.0 + 0.02 * jax.random.normal(ks[12], (D_MODEL,), f32)
    return {"x": x, "norm1_g": norm1_g, "w_in": w_in, "rel_bias": rel_bias,
            "norm_a_g": norm_a_g, "norm_b_g": norm_b_g, "w_out": w_out,
            "norm2_g": norm2_g, "w_ffn_in": w_ffn_in, "conv_w": conv_w,
            "conv_b": conv_b, "w_ffn_out": w_ffn_out, "final_g": final_g}


def reference(x, norm1_g, w_in, rel_bias, norm_a_g, norm_b_g, w_out,
              norm2_g, w_ffn_in, conv_w, conv_b, w_ffn_out, final_g):
    B, S, _ = x.shape
    for l in range(DEPTH):
        h = rmsnorm(x, norm1_g[l])
        proj = h @ w_in[l]
        qa, ka, va, qb, kb, vb = jnp.split(
            proj, [WIDTH_A, 2 * WIDTH_A, 3 * WIDTH_A,
                   3 * WIDTH_A + WIDTH_B, 3 * WIDTH_A + 2 * WIDTH_B], axis=-1)
        to_heads = lambda t, nh: t.reshape(B, S, nh, HEAD_DIM)
        ya = chunked_attention(to_heads(qa, N_HEADS_A), to_heads(ka, N_HEADS_A),
                               to_heads(va, N_HEADS_A), rel_bias[l])
        yb = stick_breaking_attention(to_heads(qb, N_HEADS_B), to_heads(kb, N_HEADS_B),
                                      to_heads(vb, N_HEADS_B))
        mixed = jnp.concatenate([rmsnorm(ya, norm_a_g[l]), rmsnorm(yb, norm_b_g[l])], axis=-1)
        x = x + mixed @ w_out[l]
        x = x + conv_gated_mlp(rmsnorm(x, norm2_g[l]), w_ffn_in[l], conv_w[l], conv_b[l], w_ffn_out[l])
    return rmsnorm(x, final_g)
```

```python
import functools

import jax
import jax.numpy as jnp
import numpy as np
from jax import lax
from jax.experimental import pallas as pl
from jax.experimental.pallas import tpu as pltpu

D_MODEL = 1024
CHUNK = 64
N_LEFT_CHUNKS = 8
HEAD_DIM = 64
N_HEADS_A = 8
N_HEADS_B = 8
WIDTH_A = N_HEADS_A * HEAD_DIM
WIDTH_B = N_HEADS_B * HEAD_DIM
MIX_WIDTH = WIDTH_A + WIDTH_B
MAX_REL = 128
D_FF = 2816
CONV_WIDTH = 3
EPS = 1e-6
NEG_INF = -1e30

LANES = 128
SUBLANES = 8
HEAD_PAIR = 2 * HEAD_DIM
assert HEAD_PAIR == LANES

PAD_ROWS = N_LEFT_CHUNKS * CHUNK
PROJ_TM = 512
assert PAD_ROWS == PROJ_TM

A_SUB = 2 * CHUNK
A_WIN = A_SUB + PAD_ROWS
A_TQ = 512

B_TQ = 256
FFN_TM = 512
FFN_CHUNK = 256
N_FF_CHUNKS = D_FF // FFN_CHUNK
assert N_FF_CHUNKS * FFN_CHUNK == D_FF

VMEM_LIMIT = 56 * 1024 * 1024


def _rms(x, g):
    return x * lax.rsqrt(jnp.mean(x * x, axis=-1, keepdims=True) + EPS) * g


def _proj_kernel(x_ref, g_ref, w_ref, cs_ref, o_ref):
    i = pl.program_id(1)

    @pl.when(i == 0)
    def _():
        o_ref[...] = jnp.zeros_like(o_ref)

    @pl.when(i > 0)
    def _():
        h = _rms(x_ref[0], g_ref[...]).astype(jnp.bfloat16)
        n_out = o_ref.shape[-1]
        step = 512
        for n in range(n_out // step):
            sl = slice(n * step, (n + 1) * step)
            acc = jnp.dot(h, w_ref[:, sl], preferred_element_type=jnp.float32)
            o_ref[0, :, sl] = (acc * cs_ref[:, sl]).astype(o_ref.dtype)


def _project(x, g, w_bf, colscale):
    B, S, D = x.shape
    n_out = w_bf.shape[1]
    n_tiles = S // PROJ_TM
    return pl.pallas_call(
        _proj_kernel,
        grid=(B, n_tiles + 1),
        in_specs=[
            pl.BlockSpec((1, PROJ_TM, D), lambda b, i: (b, jnp.maximum(i - 1, 0), 0)),
            pl.BlockSpec((1, D), lambda b, i: (0, 0)),
            pl.BlockSpec((D, n_out), lambda b, i: (0, 0), pipeline_mode=pl.Buffered(1)),
            pl.BlockSpec((1, n_out), lambda b, i: (0, 0)),
        ],
        out_specs=pl.BlockSpec((1, PROJ_TM, n_out), lambda b, i: (b, i, 0)),
        out_shape=jax.ShapeDtypeStruct((B, S + PAD_ROWS, n_out), jnp.bfloat16),
        compiler_params=pltpu.CompilerParams(
            dimension_semantics=("parallel", "arbitrary"),
            vmem_limit_bytes=VMEM_LIMIT),
        name="norm_qkv_proj",
    )(x, g, w_bf, colscale)


def _attn_a_kernel(q_ref, k_ref, v_ref, bias_ref, o_ref):
    t = pl.program_id(2)
    lane = lax.broadcasted_iota(jnp.int32, (A_SUB, LANES), 1)
    first_head = lane < HEAD_DIM
    col = lax.broadcasted_iota(jnp.int32, (A_SUB, A_WIN), 1)

    def sub_tile(j, carry):
        r0 = pl.multiple_of(j * A_SUB, A_SUB)
        w0 = pl.multiple_of(t * A_TQ + j * A_SUB, A_SUB)
        q2 = q_ref[0, pl.ds(r0, A_SUB), :]
        kw = k_ref[0, pl.ds(w0, A_WIN), :]
        vw = v_ref[0, pl.ds(w0, A_WIN), :]
        in_seq = col >= (PAD_ROWS - w0)
        outs = []
        for h in range(2):
            keep = first_head if h == 0 else jnp.logical_not(first_head)
            qh = jnp.where(keep, q2, jnp.zeros_like(q2))
            s = lax.dot_general(qh, kw, (((1,), (1,)), ((), ())),
                                preferred_element_type=jnp.float32)
            s = jnp.where(in_seq, s + bias_ref[0, h], NEG_INF)
            m = jnp.max(s, axis=-1, keepdims=True)
            p = jnp.exp(s - m)
            denom = jnp.sum(p, axis=-1, keepdims=True)
            pv = jnp.dot(p.astype(jnp.bfloat16), vw, preferred_element_type=jnp.float32)
            outs.append(pv / denom)
        o_ref[0, pl.ds(r0, A_SUB), :] = jnp.where(first_head, outs[0], outs[1])
        return carry

    lax.fori_loop(0, A_TQ // A_SUB, sub_tile, 0)


def _attention_a(proj, bias_tiles, S):
    B = proj.shape[0]
    n_pairs = WIDTH_A // HEAD_PAIR
    rows = proj.shape[1]
    pad_blocks = PAD_ROWS // A_TQ
    return pl.pallas_call(
        _attn_a_kernel,
        grid=(B, n_pairs, S // A_TQ),
        in_specs=[
            pl.BlockSpec((1, A_TQ, HEAD_PAIR), lambda b, hp, t: (b, t + pad_blocks, hp)),
            pl.BlockSpec((1, rows, HEAD_PAIR), lambda b, hp, t: (b, 0, n_pairs + hp)),
            pl.BlockSpec((1, rows, HEAD_PAIR), lambda b, hp, t: (b, 0, 2 * n_pairs + hp)),
            pl.BlockSpec((1, 2, A_SUB, A_WIN), lambda b, hp, t: (hp, 0, 0, 0)),
        ],
        out_specs=pl.BlockSpec((1, A_TQ, HEAD_PAIR), lambda b, hp, t: (b, t, hp)),
        out_shape=jax.ShapeDtypeStruct((B, S, WIDTH_A), jnp.float32),
        compiler_params=pltpu.CompilerParams(
            dimension_semantics=("parallel", "parallel", "arbitrary"),
            vmem_limit_bytes=VMEM_LIMIT),
        name="chunk_attn",
    )(proj, proj, proj, bias_tiles)


def _attn_b_kernel(q_ref, k_ref, v_ref, tri_ref, o_ref):
    qi = pl.program_id(2)
    lane = lax.broadcasted_iota(jnp.int32, (B_TQ, LANES), 1)
    first_head = lane < HEAD_DIM
    q2 = q_ref[0]
    zero = jnp.zeros_like(q2)
    qs = (jnp.where(first_head, q2, zero), jnp.where(first_head, zero, q2))
    tri = tri_ref[...]
    row = lax.broadcasted_iota(jnp.int32, (B_TQ, B_TQ), 0)
    colk = lax.broadcasted_iota(jnp.int32, (B_TQ, B_TQ), 1)
    causal = colk < row

    def block(kb, carries, acc, diagonal):
        k0 = pl.multiple_of(PAD_ROWS + kb * B_TQ, B_TQ)
        kblk = k_ref[0, pl.ds(k0, B_TQ), :]
        vblk = v_ref[0, pl.ds(k0, B_TQ), :]
        new_carries = []
        pvs = []
        for h in range(2):
            z = lax.dot_general(qs[h], kblk, (((1,), (1,)), ((), ())),
                                preferred_element_type=jnp.float32)
            log_1m = jnp.minimum(-z, 0.0) - jnp.log(1.0 + jnp.exp(-jnp.abs(z)))
            if diagonal:
                log_1m = jnp.where(causal, log_1m, 0.0)
            incl = jnp.dot(log_1m.astype(jnp.bfloat16), tri,
                           preferred_element_type=jnp.float32)
            a = jnp.exp(z + incl + carries[h])
            if diagonal:
                a = jnp.where(causal, a, 0.0)
            pvs.append(jnp.dot(a.astype(jnp.bfloat16), vblk,
                               preferred_element_type=jnp.float32))
            new_carries.append(carries[h] + incl[:, 0:1])
        acc = acc + jnp.where(first_head, pvs[0], pvs[1])
        return tuple(new_carries), acc

    c0 = jnp.zeros((B_TQ, 1), jnp.float32)
    carries, acc = block(qi, (c0, c0), jnp.zeros((B_TQ, LANES), jnp.float32), True)

    def body(j, state):
        ca, cb, acc = state
        (ca, cb), acc = block(qi - 1 - j, (ca, cb), acc, False)
        return ca, cb, acc

    _, _, acc = lax.fori_loop(0, qi, body, (carries[0], carries[1], acc))
    o_ref[0] = acc


def _attention_b(proj, tri, S):
    B = proj.shape[0]
    rows = proj.shape[1]
    n_pairs = WIDTH_B // HEAD_PAIR
    col0 = 3 * WIDTH_A // HEAD_PAIR
    pad_blocks = PAD_ROWS // B_TQ
    return pl.pallas_call(
        _attn_b_kernel,
        grid=(B, n_pairs, S // B_TQ),
        in_specs=[
            pl.BlockSpec((1, B_TQ, HEAD_PAIR), lambda b, hp, t: (b, t + pad_blocks, col0 + hp)),
            pl.BlockSpec((1, rows, HEAD_PAIR), lambda b, hp, t: (b, 0, col0 + n_pairs + hp)),
            pl.BlockSpec((1, rows, HEAD_PAIR), lambda b, hp, t: (b, 0, col0 + 2 * n_pairs + hp)),
            pl.BlockSpec((B_TQ, B_TQ), lambda b, hp, t: (0, 0)),
        ],
        out_specs=pl.BlockSpec((1, B_TQ, HEAD_PAIR), lambda b, hp, t: (b, t, hp)),
        out_shape=jax.ShapeDtypeStruct((B, S, WIDTH_B), jnp.float32),
        compiler_params=pltpu.CompilerParams(
            dimension_semantics=("parallel", "parallel", "arbitrary"),
            vmem_limit_bytes=VMEM_LIMIT),
        name="stick_breaking_attn",
    )(proj, proj, proj, tri)


def _ffn_kernel(x_ref, ya_ref, yb_ref, ga_ref, gb_ref, wo_ref, g2_ref, wi_ref,
                cw_ref, cb_ref, wf_ref, gf_ref, o_ref,
                x1_ref, h_ref, acc_ref, ubuf_ref, tail_ref):
    i = pl.program_id(1)
    tm = x_ref.shape[1]

    na = _rms(ya_ref[0], ga_ref[...]).astype(jnp.bfloat16)
    nb = _rms(yb_ref[0], gb_ref[...]).astype(jnp.bfloat16)
    mixed = (jnp.dot(na, wo_ref[0:WIDTH_A, :], preferred_element_type=jnp.float32)
             + jnp.dot(nb, wo_ref[WIDTH_A:MIX_WIDTH, :], preferred_element_type=jnp.float32))
    x1 = x_ref[0] + mixed
    x1_ref[...] = x1
    h_ref[...] = _rms(x1, g2_ref[...]).astype(jnp.bfloat16)
    acc_ref[...] = jnp.zeros_like(acc_ref)

    @pl.when(i == 0)
    def _():
        tail_ref[...] = jnp.zeros_like(tail_ref)

    def conv(u, idx, slot):
        ubuf_ref[slot, 0:SUBLANES, :] = tail_ref[idx]
        ubuf_ref[slot, SUBLANES:SUBLANES + tm, :] = u
        tail_ref[idx] = u[tm - SUBLANES:tm, :]
        u1 = ubuf_ref[slot, SUBLANES - 1:SUBLANES - 1 + tm, :]
        u2 = ubuf_ref[slot, SUBLANES - 2:SUBLANES - 2 + tm, :]
        w = cw_ref[idx]
        return w[0:1, :] * u2 + w[1:2, :] * u1 + w[2:3, :] * u + cb_ref[idx]

    def chunk(c, carry):
        h = h_ref[...]
        gate = conv(jnp.dot(h, wi_ref[c], preferred_element_type=jnp.float32), c, 0)
        val = conv(jnp.dot(h, wi_ref[N_FF_CHUNKS + c], preferred_element_type=jnp.float32),
                   N_FF_CHUNKS + c, 1)
        act = (gate * jax.nn.sigmoid(gate) * val).astype(jnp.bfloat16)
        acc_ref[...] += jnp.dot(act, wf_ref[c], preferred_element_type=jnp.float32)
        return carry

    lax.fori_loop(0, N_FF_CHUNKS, chunk, 0)
    o_ref[0] = _rms(x1_ref[...] + acc_ref[...], gf_ref[...])


def _resident(shape):
    nd = len(shape)
    return pl.BlockSpec(shape, lambda b, i: (0,) * nd, pipeline_mode=pl.Buffered(1))


def _out_ffn(x, ya, yb, ga, gb, wo_bf, g2, wi_chunks, cw_chunks, cb_chunks, wf_chunks, gf):
    B, S, D = x.shape
    tm = FFN_TM
    row = lambda w: pl.BlockSpec((1, tm, w), lambda b, i: (b, i, 0))
    return pl.pallas_call(
        _ffn_kernel,
        grid=(B, S // tm),
        in_specs=[
            row(D), row(WIDTH_A), row(WIDTH_B),
            _resident(ga.shape), _resident(gb.shape), _resident(wo_bf.shape),
            _resident(g2.shape), _resident(wi_chunks.shape), _resident(cw_chunks.shape),
            _resident(cb_chunks.shape), _resident(wf_chunks.shape), _resident(gf.shape),
        ],
        out_specs=row(D),
        out_shape=jax.ShapeDtypeStruct((B, S, D), x.dtype),
        scratch_shapes=[
            pltpu.VMEM((tm, D), jnp.float32),
            pltpu.VMEM((tm, D), jnp.bfloat16),
            pltpu.VMEM((tm, D), jnp.float32),
            pltpu.VMEM((2, tm + SUBLANES, FFN_CHUNK), jnp.float32),
            pltpu.VMEM((2 * N_FF_CHUNKS, SUBLANES, FFN_CHUNK), jnp.float32),
        ],
        compiler_params=pltpu.CompilerParams(
            dimension_semantics=("parallel", "arbitrary"),
            vmem_limit_bytes=VMEM_LIMIT),
        name="outproj_convffn",
    )(x, ya, yb, ga, gb, wo_bf, g2, wi_chunks, cw_chunks, cb_chunks, wf_chunks, gf)


def _bias_tiles(rel_bias):
    qi = np.arange(A_SUB)[:, None]
    kj = np.arange(A_WIN)[None, :]
    dist = PAD_ROWS + qi - kj
    rel_idx = np.clip(dist, -MAX_REL, MAX_REL) + MAX_REL
    q_chunk = qi // CHUNK
    k_chunk = kj // CHUNK
    in_band = (k_chunk >= q_chunk) & (k_chunk <= q_chunk + N_LEFT_CHUNKS)
    tiles = jnp.where(in_band[None], rel_bias[:, rel_idx].astype(jnp.float32), NEG_INF)
    return tiles.reshape(N_HEADS_A // 2, 2, A_SUB, A_WIN)


def _layer(x, norm1_g, w_in, rel_bias, norm_a_g, norm_b_g, w_out,
           norm2_g, w_ffn_in, conv_w, conv_b, w_ffn_out, out_g):
    B, S, D = x.shape
    bf = jnp.bfloat16
    scale = HEAD_DIM ** -0.5
    colscale = jnp.ones((3 * MIX_WIDTH,), jnp.float32)
    colscale = colscale.at[0:WIDTH_A].set(scale)
    colscale = colscale.at[3 * WIDTH_A:3 * WIDTH_A + WIDTH_B].set(scale)

    proj = _project(x, norm1_g[None, :], w_in.astype(bf), colscale[None, :])
    ya = _attention_a(proj, _bias_tiles(rel_bias), S)
    tri = jnp.asarray(np.tril(np.ones((B_TQ, B_TQ), np.float32)), bf)
    yb = _attention_b(proj, tri, S)

    c2 = 2 * N_FF_CHUNKS
    wi_chunks = w_ffn_in.astype(bf).reshape(D, c2, FFN_CHUNK).transpose(1, 0, 2)
    cw_chunks = jnp.pad(conv_w.reshape(CONV_WIDTH, c2, FFN_CHUNK).transpose(1, 0, 2),
                        ((0, 0), (0, SUBLANES - CONV_WIDTH), (0, 0)))
    cb_chunks = conv_b.reshape(c2, 1, FFN_CHUNK)
    wf_chunks = w_ffn_out.astype(bf).reshape(N_FF_CHUNKS, FFN_CHUNK, D)
    return _out_ffn(x, ya, yb, norm_a_g[None, :], norm_b_g[None, :], w_out.astype(bf),
                    norm2_g[None, :], wi_chunks, cw_chunks, cb_chunks, wf_chunks,
                    out_g[None, :])


@jax.jit
def kernel(x, norm1_g, w_in, rel_bias, norm_a_g, norm_b_g, w_out, norm2_g, w_ffn_in,
           conv_w, conv_b, w_ffn_out, final_g):
    depth = w_in.shape[0]
    assert depth == 1, "the fused output kernel applies the final norm inside the layer"
    return _layer(x, norm1_g[0], w_in[0], rel_bias[0], norm_a_g[0], norm_b_g[0], w_out[0],
                  norm2_g[0], w_ffn_in[0], conv_w[0], conv_b[0], w_ffn_out[0], final_g)
```

```python
import math

import jax
import jax.numpy as jnp
import numpy as np
from jax import lax
from jax.experimental import pallas as pl
from jax.experimental.pallas import tpu as pltpu

D_MODEL = 1024
CHUNK = 64
N_LEFT_CHUNKS = 8
HEAD_DIM = 64
N_HEADS_A = 8
N_HEADS_B = 8
WIDTH_A = N_HEADS_A * HEAD_DIM
WIDTH_B = N_HEADS_B * HEAD_DIM
MIX_WIDTH = WIDTH_A + WIDTH_B
MAX_REL = 128
D_FF = 2816
CONV_WIDTH = 3
EPS = 1e-6
NEG_INF = -1e30
LOG2E = math.log2(math.e)

LANES = 128
SUBLANES = 8
HEAD_PAIR = 2 * HEAD_DIM
assert HEAD_PAIR == LANES

PAD_ROWS = N_LEFT_CHUNKS * CHUNK
PROJ_TM = 512
assert PAD_ROWS == PROJ_TM

A_SUB = 2 * CHUNK
A_WIN = A_SUB + PAD_ROWS
A_UNROLL = 7

B_TQ = 256
B_DEAD_LOG2 = -180.0

FFN_TM = 512
FFN_CHUNK = 256
N_FF_CHUNKS = D_FF // FFN_CHUNK
assert N_FF_CHUNKS * FFN_CHUNK == D_FF

VMEM_LIMIT = 56 * 1024 * 1024


def _rms(x, g):
    return x * lax.rsqrt(jnp.mean(x * x, axis=-1, keepdims=True) + EPS) * g


def _stack_heads(q2):
    lane = lax.broadcasted_iota(jnp.int32, q2.shape, 1)
    zero = jnp.zeros_like(q2)
    return jnp.concatenate([jnp.where(lane < HEAD_DIM, q2, zero),
                            jnp.where(lane < HEAD_DIM, zero, q2)], axis=0)


def _merge_heads(pv):
    rows = pv.shape[0] // 2
    lane = lax.broadcasted_iota(jnp.int32, (rows, LANES), 1)
    return jnp.where(lane < HEAD_DIM, pv[:rows], pv[rows:])


def _proj_kernel(x_ref, g_ref, w_ref, cs_ref, o_ref):
    i = pl.program_id(1)

    @pl.when(i == 0)
    def _():
        o_ref[...] = jnp.zeros_like(o_ref)

    @pl.when(i > 0)
    def _():
        h = _rms(x_ref[0], g_ref[...]).astype(jnp.bfloat16)
        n_out = o_ref.shape[-1]
        step = 512
        for n in range(n_out // step):
            sl = slice(n * step, (n + 1) * step)
            acc = jnp.dot(h, w_ref[:, sl], preferred_element_type=jnp.float32)
            o_ref[0, :, sl] = (acc * cs_ref[:, sl]).astype(o_ref.dtype)


def _project(x, g, w_bf, colscale):
    B, S, D = x.shape
    n_out = w_bf.shape[1]
    n_tiles = S // PROJ_TM
    return pl.pallas_call(
        _proj_kernel,
        grid=(B, n_tiles + 1),
        in_specs=[
            pl.BlockSpec((1, PROJ_TM, D), lambda b, i: (b, jnp.maximum(i - 1, 0), 0)),
            pl.BlockSpec((1, D), lambda b, i: (0, 0)),
            pl.BlockSpec((D, n_out), lambda b, i: (0, 0), pipeline_mode=pl.Buffered(1)),
            pl.BlockSpec((1, n_out), lambda b, i: (0, 0)),
        ],
        out_specs=pl.BlockSpec((1, PROJ_TM, n_out), lambda b, i: (b, i, 0)),
        out_shape=jax.ShapeDtypeStruct((B, S + PAD_ROWS, n_out), jnp.bfloat16),
        compiler_params=pltpu.CompilerParams(
            dimension_semantics=("parallel", "arbitrary"),
            vmem_limit_bytes=VMEM_LIMIT),
        name="norm_qkv_proj",
    )(x, g, w_bf, colscale)


def _attn_a_kernel(q_ref, k_ref, v_ref, bias_ref, o_ref):
    n_sub = o_ref.shape[1] // A_SUB
    n_pad_sub = PAD_ROWS // A_SUB
    col = lax.broadcasted_iota(jnp.int32, (2 * A_SUB, A_WIN), 1)

    def sub_tile(w0, first_key_col):
        q2 = q_ref[0, pl.ds(w0 + PAD_ROWS, A_SUB), :]
        kw = k_ref[0, pl.ds(w0, A_WIN), :]
        vw = v_ref[0, pl.ds(w0, A_WIN), :]
        s = lax.dot_general(_stack_heads(q2), kw, (((1,), (1,)), ((), ())),
                            preferred_element_type=jnp.float32) + bias_ref[0]
        if first_key_col > 0:
            s = jnp.where(col >= first_key_col, s, NEG_INF)
        m = jnp.max(s, axis=-1, keepdims=True)
        p = jnp.exp2(s - m)
        denom = jnp.sum(p, axis=-1, keepdims=True)
        pv = jnp.dot(p.astype(jnp.bfloat16), vw, preferred_element_type=jnp.float32)
        o_ref[0, pl.ds(w0, A_SUB), :] = _merge_heads(pv / denom)

    for j in range(n_pad_sub):
        sub_tile(j * A_SUB, PAD_ROWS - j * A_SUB)

    def body(j, carry):
        sub_tile(pl.multiple_of(j * A_SUB, A_SUB), 0)
        return carry

    lax.fori_loop(n_pad_sub, n_sub, body, 0, unroll=A_UNROLL)


def _attention_a(proj, bias_tiles, S):
    B = proj.shape[0]
    n_pairs = WIDTH_A // HEAD_PAIR
    rows = proj.shape[1]
    col_block = lambda off: pl.BlockSpec((1, rows, HEAD_PAIR), lambda b, hp: (b, 0, off + hp))
    return pl.pallas_call(
        _attn_a_kernel,
        grid=(B, n_pairs),
        in_specs=[
            col_block(0), col_block(n_pairs), col_block(2 * n_pairs),
            pl.BlockSpec((1, 2 * A_SUB, A_WIN), lambda b, hp: (hp, 0, 0)),
        ],
        out_specs=pl.BlockSpec((1, S, HEAD_PAIR), lambda b, hp: (b, 0, hp)),
        out_shape=jax.ShapeDtypeStruct((B, S, WIDTH_A), jnp.float32),
        compiler_params=pltpu.CompilerParams(
            dimension_semantics=("parallel", "arbitrary"),
            vmem_limit_bytes=VMEM_LIMIT),
        name="chunk_attn",
    )(proj, proj, proj, bias_tiles)


def _attn_b_kernel(q_ref, k_ref, v_ref, ntri_ref, o_ref):
    n_q = o_ref.shape[1] // B_TQ
    row = lax.broadcasted_iota(jnp.int32, (2 * B_TQ, B_TQ), 0) & (B_TQ - 1)
    causal = lax.broadcasted_iota(jnp.int32, (2 * B_TQ, B_TQ), 1) < row
    sign_bit = jnp.int32(-2 ** 31)

    def block(qs, kb, carry, diagonal):
        k0 = pl.multiple_of(PAD_ROWS + kb * B_TQ, B_TQ)
        kblk = k_ref[0, pl.ds(k0, B_TQ), :]
        vblk = v_ref[0, pl.ds(k0, B_TQ), :]
        z = lax.dot_general(qs, kblk, (((1,), (1,)), ((), ())),
                            preferred_element_type=jnp.float32)
        neg_abs = pltpu.bitcast(pltpu.bitcast(z, jnp.int32) | sign_bit, jnp.float32)
        sp = jnp.maximum(z, 0.0) + jnp.log2(1.0 + jnp.exp2(neg_abs))
        if diagonal:
            sp = jnp.where(causal, sp, 0.0)
        incl = jnp.dot(sp.astype(jnp.bfloat16), ntri_ref[...],
                       preferred_element_type=jnp.float32)
        e = z + incl
        if carry is not None:
            e = e + carry
        a = jnp.exp2(e)
        if diagonal:
            a = jnp.where(causal, a, 0.0)
        pv = jnp.dot(a.astype(jnp.bfloat16), vblk, preferred_element_type=jnp.float32)
        return pv, incl[:, 0:1]

    def load_q(qi):
        q0 = pl.multiple_of(PAD_ROWS + qi * B_TQ, B_TQ)
        return _stack_heads(q_ref[0, pl.ds(q0, B_TQ), :])

    pv, _ = block(load_q(0), 0, None, True)
    o_ref[0, 0:B_TQ, :] = _merge_heads(pv)

    def q_tile(qi, _):
        qs = load_q(qi)
        pv_d, c_d = block(qs, qi, None, True)
        pv_l, c_l = block(qs, qi - 1, c_d, False)
        carry = c_d + c_l
        acc = pv_d + pv_l

        def cond(state):
            kb, top, _, _ = state
            return jnp.logical_and(kb >= 0, top > B_DEAD_LOG2)

        def body(state):
            kb, _, carry, acc = state
            pv, c = block(qs, kb, carry, False)
            carry = carry + c
            return kb - 1, jnp.max(carry), carry, acc + pv

        _, _, _, acc = lax.while_loop(cond, body, (qi - 2, jnp.max(carry), carry, acc))
        o_ref[0, pl.ds(pl.multiple_of(qi * B_TQ, B_TQ), B_TQ), :] = _merge_heads(acc)
        return 0

    lax.fori_loop(1, n_q, q_tile, 0)


def _attention_b(proj, ntri, S):
    B = proj.shape[0]
    rows = proj.shape[1]
    n_pairs = WIDTH_B // HEAD_PAIR
    col0 = 3 * WIDTH_A // HEAD_PAIR
    col_block = lambda off: pl.BlockSpec((1, rows, HEAD_PAIR), lambda b, hp: (b, 0, col0 + off + hp))
    return pl.pallas_call(
        _attn_b_kernel,
        grid=(B, n_pairs),
        in_specs=[
            col_block(0), col_block(n_pairs), col_block(2 * n_pairs),
            pl.BlockSpec((B_TQ, B_TQ), lambda b, hp: (0, 0)),
        ],
        out_specs=pl.BlockSpec((1, S, HEAD_PAIR), lambda b, hp: (b, 0, hp)),
        out_shape=jax.ShapeDtypeStruct((B, S, WIDTH_B), jnp.float32),
        compiler_params=pltpu.CompilerParams(
            dimension_semantics=("parallel", "arbitrary"),
            vmem_limit_bytes=VMEM_LIMIT),
        name="stick_breaking_attn",
    )(proj, proj, proj, ntri)


def _ffn_kernel(x_ref, ya_ref, yb_ref, ga_ref, gb_ref, wo_ref, g2_ref, wi_ref,
                cw_ref, cb_ref, wf_ref, gf_ref, o_ref, tail_ref):
    i = pl.program_id(1)
    tm = x_ref.shape[1]

    na = _rms(ya_ref[0], ga_ref[...]).astype(jnp.bfloat16)
    nb = _rms(yb_ref[0], gb_ref[...]).astype(jnp.bfloat16)
    mixed = (jnp.dot(na, wo_ref[0:WIDTH_A, :], preferred_element_type=jnp.float32)
             + jnp.dot(nb, wo_ref[WIDTH_A:MIX_WIDTH, :], preferred_element_type=jnp.float32))
    x1 = x_ref[0] + mixed
    h = _rms(x1, g2_ref[...]).astype(jnp.bfloat16)

    @pl.when(i == 0)
    def _():
        tail_ref[...] = jnp.zeros_like(tail_ref)

    sub = lax.broadcasted_iota(jnp.int32, (SUBLANES, FFN_CHUNK), 0)

    def shifted(u, prev, k):
        r = pltpu.roll(u, k, axis=0)
        head = jnp.where(sub < k, pltpu.roll(prev, k, axis=0), r[0:SUBLANES])
        return jnp.concatenate([head, r[SUBLANES:]], axis=0)

    def conv(u, idx):
        prev = tail_ref[idx]
        tail_ref[idx] = u[tm - SUBLANES:tm, :]
        w = cw_ref[idx]
        return (w[0:1, :] * shifted(u, prev, 2) + w[1:2, :] * shifted(u, prev, 1)
                + w[2:3, :] * u + cb_ref[idx])

    def up(c):
        return (jnp.dot(h, wi_ref[c], preferred_element_type=jnp.float32),
                jnp.dot(h, wi_ref[N_FF_CHUNKS + c], preferred_element_type=jnp.float32))

    acc = x1
    u_next = up(0)
    act_prev = None
    for c in range(N_FF_CHUNKS):
        u_gate, u_val = u_next
        if c + 1 < N_FF_CHUNKS:
            u_next = up(c + 1)
        if act_prev is not None:
            acc = acc + jnp.dot(act_prev, wf_ref[c - 1], preferred_element_type=jnp.float32)
        gate = conv(u_gate, c)
        val = conv(u_val, N_FF_CHUNKS + c)
        act_prev = (gate * jax.nn.sigmoid(gate) * val).astype(jnp.bfloat16)
    acc = acc + jnp.dot(act_prev, wf_ref[N_FF_CHUNKS - 1], preferred_element_type=jnp.float32)
    o_ref[0] = _rms(acc, gf_ref[...])


def _resident(shape):
    nd = len(shape)
    return pl.BlockSpec(shape, lambda b, i: (0,) * nd, pipeline_mode=pl.Buffered(1))


def _out_ffn(x, ya, yb, ga, gb, wo_bf, g2, wi_chunks, cw_chunks, cb_chunks, wf_chunks, gf):
    B, S, D = x.shape
    tm = FFN_TM
    row = lambda w: pl.BlockSpec((1, tm, w), lambda b, i: (b, i, 0))
    return pl.pallas_call(
        _ffn_kernel,
        grid=(B, S // tm),
        in_specs=[
            row(D), row(WIDTH_A), row(WIDTH_B),
            _resident(ga.shape), _resident(gb.shape), _resident(wo_bf.shape),
            _resident(g2.shape), _resident(wi_chunks.shape), _resident(cw_chunks.shape),
            _resident(cb_chunks.shape), _resident(wf_chunks.shape), _resident(gf.shape),
        ],
        out_specs=row(D),
        out_shape=jax.ShapeDtypeStruct((B, S, D), x.dtype),
        scratch_shapes=[
            pltpu.VMEM((2 * N_FF_CHUNKS, SUBLANES, FFN_CHUNK), jnp.float32),
        ],
        compiler_params=pltpu.CompilerParams(
            dimension_semantics=("parallel", "arbitrary"),
            vmem_limit_bytes=VMEM_LIMIT),
        name="outproj_convffn",
    )(x, ya, yb, ga, gb, wo_bf, g2, wi_chunks, cw_chunks, cb_chunks, wf_chunks, gf)


def _bias_tiles(rel_bias):
    L = A_WIN + A_SUB
    m = np.arange(L)
    jm = np.where(m < A_WIN, m, m - L)
    idx = np.clip(PAD_ROWS - jm, -MAX_REL, MAX_REL) + MAX_REL
    v = rel_bias[:, idx].astype(jnp.float32) * LOG2E
    H = rel_bias.shape[0]
    toep = jnp.tile(v, (1, A_SUB))[:, :A_SUB * (L - 1)].reshape(H, A_SUB, L - 1)[:, :, :A_WIN]
    q_chunk = np.arange(A_SUB)[:, None] // CHUNK
    k_chunk = np.arange(A_WIN)[None, :] // CHUNK
    in_band = (k_chunk >= q_chunk) & (k_chunk <= q_chunk + N_LEFT_CHUNKS)
    tiles = jnp.where(in_band[None], toep, NEG_INF)
    return tiles.reshape(H // 2, 2 * A_SUB, A_WIN)


def _layer(x, norm1_g, w_in, rel_bias, norm_a_g, norm_b_g, w_out,
           norm2_g, w_ffn_in, conv_w, conv_b, w_ffn_out, out_g):
    B, S, D = x.shape
    bf = jnp.bfloat16
    q_scale = HEAD_DIM ** -0.5 * LOG2E
    colscale = jnp.ones((3 * MIX_WIDTH,), jnp.float32)
    colscale = colscale.at[0:WIDTH_A].set(q_scale)
    colscale = colscale.at[3 * WIDTH_A:3 * WIDTH_A + WIDTH_B].set(q_scale)

    proj = _project(x, norm1_g[None, :], w_in.astype(bf), colscale[None, :])
    ya = _attention_a(proj, _bias_tiles(rel_bias), S)
    ntri = jnp.asarray(-np.tril(np.ones((B_TQ, B_TQ), np.float32)), bf)
    yb = _attention_b(proj, ntri, S)

    c2 = 2 * N_FF_CHUNKS
    wi_chunks = w_ffn_in.astype(bf).reshape(D, c2, FFN_CHUNK).transpose(1, 0, 2)
    cw_chunks = jnp.pad(conv_w.reshape(CONV_WIDTH, c2, FFN_CHUNK).transpose(1, 0, 2),
                        ((0, 0), (0, SUBLANES - CONV_WIDTH), (0, 0)))
    cb_chunks = conv_b.reshape(c2, 1, FFN_CHUNK)
    wf_chunks = w_ffn_out.astype(bf).reshape(N_FF_CHUNKS, FFN_CHUNK, D)
    return _out_ffn(x, ya, yb, norm_a_g[None, :], norm_b_g[None, :], w_out.astype(bf),
                    norm2_g[None, :], wi_chunks, cw_chunks, cb_chunks, wf_chunks,
                    out_g[None, :])


@jax.jit
def kernel(x, norm1_g, w_in, rel_bias, norm_a_g, norm_b_g, w_out, norm2_g, w_ffn_in,
           conv_w, conv_b, w_ffn_out, final_g):
    depth = w_in.shape[0]
    assert depth == 1, "the fused output kernel applies the final norm inside the layer"
    return _layer(x, norm1_g[0], w_in[0], rel_bias[0], norm_a_g[0], norm_b_g[0], w_out[0],
                  norm2_g[0], w_ffn_in[0], conv_w[0], conv_b[0], w_ffn_out[0], final_g)
```

```python
import math

import jax
import jax.numpy as jnp
import numpy as np
from jax import lax
from jax.experimental import pallas as pl
from jax.experimental.pallas import tpu as pltpu

D_MODEL = 1024
CHUNK = 64
N_LEFT_CHUNKS = 8
HEAD_DIM = 64
N_HEADS_A = 8
N_HEADS_B = 8
WIDTH_A = N_HEADS_A * HEAD_DIM
WIDTH_B = N_HEADS_B * HEAD_DIM
MIX_WIDTH = WIDTH_A + WIDTH_B
MAX_REL = 128
D_FF = 2816
CONV_WIDTH = 3
EPS = 1e-6
NEG_INF = -1e30
LOG2E = math.log2(math.e)

LANES = 128
SUBLANES = 8
HEAD_PAIR = 2 * HEAD_DIM
assert HEAD_PAIR == LANES

PAD_ROWS = N_LEFT_CHUNKS * CHUNK
PROJ_TM = 512
assert PAD_ROWS == PROJ_TM

A_SUB = 2 * CHUNK
A_WIN = A_SUB + PAD_ROWS
A_GROUP = 4

B_TQ = 256
B_DEAD_LOG2 = -180.0

FFN_TM = 512
FFN_CHUNK = 256
N_FF_CHUNKS = D_FF // FFN_CHUNK
assert N_FF_CHUNKS * FFN_CHUNK == D_FF

VMEM_LIMIT = 56 * 1024 * 1024


def _rms(x, g):
    return x * lax.rsqrt(jnp.mean(x * x, axis=-1, keepdims=True) + EPS) * g


def _stack_heads(q2):
    lane = lax.broadcasted_iota(jnp.int32, q2.shape, 1)
    zero = jnp.zeros_like(q2)
    return jnp.concatenate([jnp.where(lane < HEAD_DIM, q2, zero),
                            jnp.where(lane < HEAD_DIM, zero, q2)], axis=0)


def _merge_heads(pv):
    rows = pv.shape[0] // 2
    lane = lax.broadcasted_iota(jnp.int32, (rows, LANES), 1)
    return jnp.where(lane < HEAD_DIM, pv[:rows], pv[rows:])


def _proj_kernel(x_ref, g_ref, w_ref, cs_ref, o_ref):
    i = pl.program_id(1)

    @pl.when(i == 0)
    def _():
        o_ref[...] = jnp.zeros_like(o_ref)

    @pl.when(i > 0)
    def _():
        h = _rms(x_ref[0], g_ref[...]).astype(jnp.bfloat16)
        n_out = o_ref.shape[-1]
        step = 512
        for n in range(n_out // step):
            sl = slice(n * step, (n + 1) * step)
            acc = jnp.dot(h, w_ref[:, sl], preferred_element_type=jnp.float32)
            o_ref[0, :, sl] = (acc * cs_ref[:, sl]).astype(o_ref.dtype)


def _project(x, g, w_bf, colscale):
    B, S, D = x.shape
    n_out = w_bf.shape[1]
    n_tiles = S // PROJ_TM
    return pl.pallas_call(
        _proj_kernel,
        grid=(B, n_tiles + 1),
        in_specs=[
            pl.BlockSpec((1, PROJ_TM, D), lambda b, i: (b, jnp.maximum(i - 1, 0), 0)),
            pl.BlockSpec((1, D), lambda b, i: (0, 0)),
            pl.BlockSpec((D, n_out), lambda b, i: (0, 0), pipeline_mode=pl.Buffered(1)),
            pl.BlockSpec((1, n_out), lambda b, i: (0, 0)),
        ],
        out_specs=pl.BlockSpec((1, PROJ_TM, n_out), lambda b, i: (b, i, 0)),
        out_shape=jax.ShapeDtypeStruct((B, S + PAD_ROWS, n_out), jnp.bfloat16),
        compiler_params=pltpu.CompilerParams(
            dimension_semantics=("parallel", "arbitrary"),
            vmem_limit_bytes=VMEM_LIMIT),
        name="norm_qkv_proj",
    )(x, g, w_bf, colscale)


def _attn_a_kernel(q_ref, k_ref, v_ref, bias_ref, o_ref):
    n_sub = o_ref.shape[1] // A_SUB
    n_pad_sub = PAD_ROWS // A_SUB
    col = lax.broadcasted_iota(jnp.int32, (2 * A_SUB, A_WIN), 1)

    def sub_tiles(tiles):
        scores, values = [], []
        for w0, first_key_col in tiles:
            q2 = q_ref[0, pl.ds(w0 + PAD_ROWS, A_SUB), :]
            kw = k_ref[0, pl.ds(w0, A_WIN), :]
            values.append(v_ref[0, pl.ds(w0, A_WIN), :])
            s = lax.dot_general(_stack_heads(q2), kw, (((1,), (1,)), ((), ())),
                                preferred_element_type=jnp.float32) + bias_ref[0]
            if first_key_col > 0:
                s = jnp.where(col >= first_key_col, s, NEG_INF)
            scores.append(s)
        probs, denoms = [], []
        for s in scores:
            m = jnp.max(s, axis=-1, keepdims=True)
            p = jnp.exp2(s - m)
            denoms.append(jnp.sum(p, axis=-1, keepdims=True))
            probs.append(p.astype(jnp.bfloat16))
        for (w0, _), p, vw, denom in zip(tiles, probs, values, denoms):
            pv = jnp.dot(p, vw, preferred_element_type=jnp.float32)
            o_ref[0, pl.ds(w0, A_SUB), :] = _merge_heads(pv / denom).astype(o_ref.dtype)

    sub_tiles([(j * A_SUB, PAD_ROWS - j * A_SUB) for j in range(n_pad_sub)])

    def body(g, carry):
        j0 = n_pad_sub + g * A_GROUP
        sub_tiles([(pl.multiple_of((j0 + j) * A_SUB, A_SUB), 0) for j in range(A_GROUP)])
        return carry

    assert (n_sub - n_pad_sub) % A_GROUP == 0
    lax.fori_loop(0, (n_sub - n_pad_sub) // A_GROUP, body, 0)


def _attention_a(proj, bias_tiles, S):
    B = proj.shape[0]
    n_pairs = WIDTH_A // HEAD_PAIR
    rows = proj.shape[1]
    col_block = lambda off: pl.BlockSpec((1, rows, HEAD_PAIR), lambda b, hp: (b, 0, off + hp))
    return pl.pallas_call(
        _attn_a_kernel,
        grid=(B, n_pairs),
        in_specs=[
            col_block(0), col_block(n_pairs), col_block(2 * n_pairs),
            pl.BlockSpec((1, 2 * A_SUB, A_WIN), lambda b, hp: (hp, 0, 0)),
        ],
        out_specs=pl.BlockSpec((1, S, HEAD_PAIR), lambda b, hp: (b, 0, hp)),
        out_shape=jax.ShapeDtypeStruct((B, S, WIDTH_A), jnp.bfloat16),
        compiler_params=pltpu.CompilerParams(
            dimension_semantics=("parallel", "arbitrary"),
            vmem_limit_bytes=VMEM_LIMIT),
        name="chunk_attn",
    )(proj, proj, proj, bias_tiles)


def _attn_b_kernel(q_ref, k_ref, v_ref, ntri_ref, o_ref):
    n_q = o_ref.shape[1] // B_TQ
    row = lax.broadcasted_iota(jnp.int32, (2 * B_TQ, B_TQ), 0) & (B_TQ - 1)
    causal = lax.broadcasted_iota(jnp.int32, (2 * B_TQ, B_TQ), 1) < row
    sign_bit = jnp.int32(-2 ** 31)

    def load_q(qi):
        q0 = pl.multiple_of(PAD_ROWS + qi * B_TQ, B_TQ)
        return _stack_heads(q_ref[0, pl.ds(q0, B_TQ), :])

    def softplus2(z):
        neg_abs = pltpu.bitcast(pltpu.bitcast(z, jnp.int32) | sign_bit, jnp.float32)
        return jnp.maximum(z, 0.0) + jnp.log2(1.0 + jnp.exp2(neg_abs))

    def block(kb, tiles):
        k0 = pl.multiple_of(PAD_ROWS + kb * B_TQ, B_TQ)
        kblk = k_ref[0, pl.ds(k0, B_TQ), :]
        vblk = v_ref[0, pl.ds(k0, B_TQ), :]
        zs = [lax.dot_general(qs, kblk, (((1,), (1,)), ((), ())),
                              preferred_element_type=jnp.float32) for qs, _, _ in tiles]
        incls = []
        for z, (_, _, diagonal) in zip(zs, tiles):
            sp = softplus2(z)
            if diagonal:
                sp = jnp.where(causal, sp, 0.0)
            incls.append(jnp.dot(sp.astype(jnp.bfloat16), ntri_ref[...],
                                 preferred_element_type=jnp.float32))
        outs = []
        for z, incl, (_, carry, diagonal) in zip(zs, incls, tiles):
            e = z + incl
            if carry is not None:
                e = e + carry
            a = jnp.exp2(e)
            if diagonal:
                a = jnp.where(causal, a, 0.0)
            pv = jnp.dot(a.astype(jnp.bfloat16), vblk, preferred_element_type=jnp.float32)
            outs.append((pv, incl[:, 0:1]))
        return outs

    def finish(qi, qs, carry, acc):
        def cond(state):
            kb, top, _, _ = state
            return jnp.logical_and(kb >= 0, top > B_DEAD_LOG2)

        def body(state):
            kb, _, carry, acc = state
            (pv, c), = block(kb, [(qs, carry, False)])
            carry = carry + c
            return kb - 1, jnp.max(carry), carry, acc + pv

        _, _, _, acc = lax.while_loop(cond, body, (qi - 2, jnp.max(carry), carry, acc))
        r0 = pl.multiple_of(qi * B_TQ, B_TQ)
        o_ref[0, pl.ds(r0, B_TQ), :] = _merge_heads(acc).astype(o_ref.dtype)

    (pv_d, c_d), = block(n_q - 1, [(load_q(n_q - 1), None, True)])

    def step(i, state):
        pv_d, c_d = state
        kb = n_q - 2 - i
        qs_right = load_q(kb + 1)
        (pv_t, c_t), (pv_r, c_r) = block(kb, [(load_q(kb), None, True), (qs_right, c_d, False)])
        finish(kb + 1, qs_right, c_d + c_r, pv_d + pv_r)
        return pv_t, c_t

    pv_d, _ = lax.fori_loop(0, n_q - 1, step, (pv_d, c_d))
    o_ref[0, 0:B_TQ, :] = _merge_heads(pv_d).astype(o_ref.dtype)


def _attention_b(proj, ntri, S):
    B = proj.shape[0]
    rows = proj.shape[1]
    n_pairs = WIDTH_B // HEAD_PAIR
    col0 = 3 * WIDTH_A // HEAD_PAIR
    col_block = lambda off: pl.BlockSpec((1, rows, HEAD_PAIR), lambda b, hp: (b, 0, col0 + off + hp))
    return pl.pallas_call(
        _attn_b_kernel,
        grid=(B, n_pairs),
        in_specs=[
            col_block(0), col_block(n_pairs), col_block(2 * n_pairs),
            pl.BlockSpec((B_TQ, B_TQ), lambda b, hp: (0, 0)),
        ],
        out_specs=pl.BlockSpec((1, S, HEAD_PAIR), lambda b, hp: (b, 0, hp)),
        out_shape=jax.ShapeDtypeStruct((B, S, WIDTH_B), jnp.bfloat16),
        compiler_params=pltpu.CompilerParams(
            dimension_semantics=("parallel", "arbitrary"),
            vmem_limit_bytes=VMEM_LIMIT),
        name="stick_breaking_attn",
    )(proj, proj, proj, ntri)


def _ffn_kernel(x_ref, ya_ref, yb_ref, ga_ref, gb_ref, wo_ref, g2_ref, wi_ref,
                cw_ref, cb_ref, wf_ref, gf_ref, o_ref, tail_ref):
    i = pl.program_id(1)
    tm = x_ref.shape[1]

    na = _rms(ya_ref[0].astype(jnp.float32), ga_ref[...]).astype(jnp.bfloat16)
    nb = _rms(yb_ref[0].astype(jnp.float32), gb_ref[...]).astype(jnp.bfloat16)
    mixed = (jnp.dot(na, wo_ref[0:WIDTH_A, :], preferred_element_type=jnp.float32)
             + jnp.dot(nb, wo_ref[WIDTH_A:MIX_WIDTH, :], preferred_element_type=jnp.float32))
    x1 = x_ref[0] + mixed
    h = _rms(x1, g2_ref[...]).astype(jnp.bfloat16)

    @pl.when(i == 0)
    def _():
        tail_ref[...] = jnp.zeros_like(tail_ref)

    sub = lax.broadcasted_iota(jnp.int32, (SUBLANES, FFN_CHUNK), 0)

    def shifted(u, prev, k):
        r = pltpu.roll(u, k, axis=0)
        head = jnp.where(sub < k, pltpu.roll(prev, k, axis=0), r[0:SUBLANES])
        return jnp.concatenate([head, r[SUBLANES:]], axis=0)

    def conv(u, idx):
        cols = slice(idx * FFN_CHUNK, (idx + 1) * FFN_CHUNK)
        prev = tail_ref[:, cols]
        tail_ref[:, cols] = u[tm - SUBLANES:tm, :]
        w = cw_ref[:, cols]
        return (w[0:1, :] * shifted(u, prev, 2) + w[1:2, :] * shifted(u, prev, 1)
                + w[2:3, :] * u + cb_ref[:, cols])

    def up(c):
        g0, v0 = c * FFN_CHUNK, D_FF + c * FFN_CHUNK
        return (jnp.dot(h, wi_ref[:, g0:g0 + FFN_CHUNK], preferred_element_type=jnp.float32),
                jnp.dot(h, wi_ref[:, v0:v0 + FFN_CHUNK], preferred_element_type=jnp.float32))

    def down(act, c):
        return jnp.dot(act, wf_ref[c * FFN_CHUNK:(c + 1) * FFN_CHUNK, :],
                       preferred_element_type=jnp.float32)

    acc = x1
    u_next = up(0)
    act_prev = None
    for c in range(N_FF_CHUNKS):
        u_gate, u_val = u_next
        if c + 1 < N_FF_CHUNKS:
            u_next = up(c + 1)
        if act_prev is not None:
            acc = acc + down(act_prev, c - 1)
        gate = conv(u_gate, c)
        val = conv(u_val, N_FF_CHUNKS + c)
        act_prev = (gate * jax.nn.sigmoid(gate) * val).astype(jnp.bfloat16)
    acc = acc + down(act_prev, N_FF_CHUNKS - 1)
    o_ref[0] = _rms(acc, gf_ref[...])


def _resident(shape):
    nd = len(shape)
    return pl.BlockSpec(shape, lambda b, i: (0,) * nd, pipeline_mode=pl.Buffered(1))


def _out_ffn(x, ya, yb, ga, gb, wo_bf, g2, wi_bf, cw, cb, wf_bf, gf):
    B, S, D = x.shape
    tm = FFN_TM
    row = lambda w: pl.BlockSpec((1, tm, w), lambda b, i: (b, i, 0))
    return pl.pallas_call(
        _ffn_kernel,
        grid=(B, S // tm),
        in_specs=[
            row(D), row(WIDTH_A), row(WIDTH_B),
            _resident(ga.shape), _resident(gb.shape), _resident(wo_bf.shape),
            _resident(g2.shape), _resident(wi_bf.shape), _resident(cw.shape),
            _resident(cb.shape), _resident(wf_bf.shape), _resident(gf.shape),
        ],
        out_specs=row(D),
        out_shape=jax.ShapeDtypeStruct((B, S, D), x.dtype),
        scratch_shapes=[
            pltpu.VMEM((SUBLANES, 2 * D_FF), jnp.float32),
        ],
        compiler_params=pltpu.CompilerParams(
            dimension_semantics=("parallel", "arbitrary"),
            vmem_limit_bytes=VMEM_LIMIT),
        name="outproj_convffn",
    )(x, ya, yb, ga, gb, wo_bf, g2, wi_bf, cw, cb, wf_bf, gf)


def _bias_tiles(rel_bias):
    L = A_WIN + A_SUB
    m = np.arange(L)
    jm = np.where(m < A_WIN, m, m - L)
    idx = np.clip(PAD_ROWS - jm, -MAX_REL, MAX_REL) + MAX_REL
    v = rel_bias[:, idx].astype(jnp.float32) * LOG2E
    H = rel_bias.shape[0]
    toep = jnp.tile(v, (1, A_SUB))[:, :A_SUB * (L - 1)].reshape(H, A_SUB, L - 1)[:, :, :A_WIN]
    q_chunk = np.arange(A_SUB)[:, None] // CHUNK
    k_chunk = np.arange(A_WIN)[None, :] // CHUNK
    in_band = (k_chunk >= q_chunk) & (k_chunk <= q_chunk + N_LEFT_CHUNKS)
    tiles = jnp.where(in_band[None], toep, NEG_INF)
    return tiles.reshape(H // 2, 2 * A_SUB, A_WIN)


def _layer(x, norm1_g, w_in, rel_bias, norm_a_g, norm_b_g, w_out,
           norm2_g, w_ffn_in, conv_w, conv_b, w_ffn_out, out_g):
    B, S, D = x.shape
    bf = jnp.bfloat16
    q_scale = HEAD_DIM ** -0.5 * LOG2E
    colscale = jnp.ones((3 * MIX_WIDTH,), jnp.float32)
    colscale = colscale.at[0:WIDTH_A].set(q_scale)
    colscale = colscale.at[3 * WIDTH_A:3 * WIDTH_A + WIDTH_B].set(q_scale)

    proj = _project(x, norm1_g[None, :], w_in.astype(bf), colscale[None, :])
    ya = _attention_a(proj, _bias_tiles(rel_bias), S)
    ntri = jnp.asarray(-np.tril(np.ones((B_TQ, B_TQ), np.float32)), bf)
    yb = _attention_b(proj, ntri, S)

    cw = jnp.pad(conv_w, ((0, SUBLANES - CONV_WIDTH), (0, 0)))
    return _out_ffn(x, ya, yb, norm_a_g[None, :], norm_b_g[None, :], w_out.astype(bf),
                    norm2_g[None, :], w_ffn_in.astype(bf), cw, conv_b[None, :],
                    w_ffn_out.astype(bf), out_g[None, :])


@jax.jit
def kernel(x, norm1_g, w_in, rel_bias, norm_a_g, norm_b_g, w_out, norm2_g, w_ffn_in,
           conv_w, conv_b, w_ffn_out, final_g):
    depth = w_in.shape[0]
    assert depth == 1, "the fused output kernel applies the final norm inside the layer"
    return _layer(x, norm1_g[0], w_in[0], rel_bias[0], norm_a_g[0], norm_b_g[0], w_out[0],
                  norm2_g[0], w_ffn_in[0], conv_w[0], conv_b[0], w_ffn_out[0], final_g)
```
